```python
import math
import jax, jax.numpy as jnp
from jax import lax
import numpy as np

D_MODEL = 1024
BATCH = 8
SEQ = 4096
DEPTH = 2

D_MIX = D_MODEL
GLA_HEADS = 4
GLA_DV = 64
GLA_DK = GLA_DV // 2
GLA_GATE_RANK = 16
GLA_TAU = 16.0
GLA_CHUNK = 64
CONV_CH = D_MIX // 4
CONV_WIDTH = 31
SWA_Q_HEADS = 8
SWA_KV_HEADS = 2
SWA_HEAD_DIM = 64
SWA_WINDOW = 128
SWA_BLOCK = 128
D_FF = 2816
RMS_EPS = 1e-6
LN_EPS = 1e-5

kernel_name = "hybrid_gla_conformer_swa_macaron"


def _in_proj_widths():
    return [
        GLA_HEADS * GLA_DK,
        GLA_HEADS * GLA_DK,
        GLA_HEADS * GLA_DV,
        GLA_HEADS * GLA_DV,
        GLA_GATE_RANK,
        CONV_CH,
        CONV_CH,
        SWA_Q_HEADS * SWA_HEAD_DIM,
        SWA_KV_HEADS * SWA_HEAD_DIM,
        SWA_KV_HEADS * SWA_HEAD_DIM,
    ]


def rmsnorm(x, g):
    x32 = x.astype(jnp.float32)
    y = x32 * lax.rsqrt(jnp.mean(x32 * x32, axis=-1, keepdims=True) + RMS_EPS)
    return (y * g.astype(jnp.float32)).astype(x.dtype)


def swiglu_ffn(h, w_gate, w_up, w_down):
    return (jax.nn.silu(h @ w_gate) * (h @ w_up)) @ w_down


def gla_group(q, k, v, glog, r, norm_g):
    B, S, H, DK = q.shape
    DV = v.shape[-1]
    C = GLA_CHUNK
    N = S // C
    f32 = jnp.float32

    def chunks(t):
        return t.astype(f32).reshape(B, N, C, H, t.shape[-1]).transpose(1, 0, 3, 2, 4)

    qc = chunks(q) * (DK ** -0.5)
    kc = chunks(k)
    vc = chunks(v)
    bc = jnp.cumsum(chunks(glog), axis=3)
    causal = jnp.tril(jnp.ones((C, C), dtype=bool))[:, :, None]

    def step(state, inp):
        qn, kn, vn, bn = inp
        diff = bn[:, :, :, None, :] - bn[:, :, None, :, :]
        decay = jnp.where(causal, jnp.exp(jnp.where(causal, diff, 0.0)), 0.0)
        attn = jnp.einsum('bhid,bhjd,bhijd->bhij', qn, kn, decay)
        o = jnp.einsum('bhij,bhjv->bhiv', attn, vn) \
            + jnp.einsum('bhid,bhdv->bhiv', qn * jnp.exp(bn), state)
        b_last = bn[:, :, -1:, :]
        state = jnp.exp(b_last[:, :, 0, :])[..., None] * state \
            + jnp.einsum('bhjd,bhjv->bhdv', kn * jnp.exp(b_last - bn), vn)
        return state, o

    s0 = jnp.zeros((B, H, DK, DV), f32)
    _, o = lax.scan(step, s0, (qc, kc, vc, bc))
    o = o.transpose(1, 0, 3, 2, 4).reshape(B, S, H, DV)
    o = o * lax.rsqrt(jnp.mean(o * o, axis=-1, keepdims=True) + RMS_EPS) * norm_g.astype(f32)
    o = o.reshape(B, S, H * DV) * jax.nn.silu(r.astype(f32))
    return o.astype(q.dtype)


def conv_group(a, g, conv_w, conv_b, ln_g, ln_b):
    u = a * jax.nn.sigmoid(g)
    y = lax.conv_general_dilated(
        u, conv_w[:, None, :].astype(u.dtype), window_strides=(1,),
        padding=[(CONV_WIDTH - 1, 0)], dimension_numbers=('NWC', 'WIO', 'NWC'),
        feature_group_count=CONV_CH) + conv_b
    y32 = y.astype(jnp.float32)
    mu = jnp.mean(y32, axis=-1, keepdims=True)
    var = jnp.mean(jnp.square(y32 - mu), axis=-1, keepdims=True)
    y32 = (y32 - mu) * lax.rsqrt(var + LN_EPS) * ln_g.astype(jnp.float32) + ln_b.astype(jnp.float32)
    return jax.nn.silu(y32).astype(a.dtype)


def swa_group(q, k, v, sinks):
    B, S, _ = q.shape
    BLK, HKV, DH = SWA_BLOCK, SWA_KV_HEADS, SWA_HEAD_DIM
    G = SWA_Q_HEADS // HKV
    nb = S // BLK
    qb = q.reshape(B, nb, BLK, HKV, G, DH)
    kb = k.reshape(B, nb, BLK, HKV, DH)
    vb = v.reshape(B, nb, BLK, HKV, DH)

    def with_prev(t):
        prev = jnp.pad(t, ((0, 0), (1, 0), (0, 0), (0, 0), (0, 0)))[:, :-1]
        return jnp.concatenate([prev, t], axis=2)

    kk, vv = with_prev(kb), with_prev(vb)
    s = jnp.einsum('bnqhgd,bnkhd->bnhgqk', qb, kk).astype(jnp.float32) * (DH ** -0.5)
    qi = jnp.arange(BLK)[:, None]
    kj = jnp.arange(2 * BLK)[None, :]
    dist = qi - kj + BLK
    blk = jnp.arange(nb)[:, None, None]
    valid = (dist >= 0) & (dist < SWA_WINDOW) & ((blk > 0) | (kj >= BLK))
    slopes = jnp.exp2(-8.0 * (jnp.arange(SWA_Q_HEADS, dtype=jnp.float32) + 1.0) / SWA_Q_HEADS)
    slopes = slopes.reshape(HKV, G)
    s = s - slopes[:, :, None, None] * dist.astype(jnp.float32)
    s = jnp.where(valid[None, :, None, None], s, -jnp.inf)
    sink = jnp.broadcast_to(sinks.astype(jnp.float32).reshape(HKV, G)[None, None, :, :, None, None],
                            s.shape[:-1] + (1,))
    p = jax.nn.softmax(jnp.concatenate([s, sink], axis=-1), axis=-1)[..., :-1]
    o = jnp.einsum('bnhgqk,bnkhd->bnqhgd', p.astype(v.dtype), vv)
    return o.reshape(B, S, SWA_Q_HEADS * DH)


def hybrid_mixer(h, w_in, gla_w_gate2, gla_b_gate, gla_norm_g, conv_w, conv_b,
                 conv_ln_g, conv_ln_b, swa_sinks, w_out):
    B, S, _ = h.shape
    p = h @ w_in
    offsets = np.cumsum(_in_proj_widths())[:-1].tolist()
    gq, gk, gv, gr, glr, ca, cg, sq, sk, sv = jnp.split(p, offsets, axis=-1)
    glog = jax.nn.log_sigmoid((glr @ gla_w_gate2 + gla_b_gate).astype(jnp.float32)) / GLA_TAU
    o_gla = gla_group(gq.reshape(B, S, GLA_HEADS, GLA_DK), gk.reshape(B, S, GLA_HEADS, GLA_DK),
                      gv.reshape(B, S, GLA_HEADS, GLA_DV), glog.reshape(B, S, GLA_HEADS, GLA_DK),
                      gr, gla_norm_g)
    o_conv = conv_group(ca, cg, conv_w, conv_b, conv_ln_g, conv_ln_b)
    o_swa = swa_group(sq, sk, sv, swa_sinks)
    return jnp.concatenate([o_gla, o_conv, o_swa], axis=-1) @ w_out


def setup_inputs(seed: int = 0) -> dict:
    key = jax.random.key(seed)
    ks = jax.random.split(key, 16)
    f32 = jnp.float32
    d_in = sum(_in_proj_widths())
    nrm = lambda k, shape, scale: jax.random.normal(k, shape, f32) * scale
    return {
        "x": jax.random.normal(ks[0], (BATCH, SEQ, D_MODEL), f32),
        "norm_g": 1.0 + nrm(ks[1], (DEPTH, 6, D_MODEL), 0.05),
        "ffn_w_gate": nrm(ks[2], (DEPTH, 2, D_MODEL, D_FF), D_MODEL ** -0.5),
        "ffn_w_up": nrm(ks[3], (DEPTH, 2, D_MODEL, D_FF), D_MODEL ** -0.5),
        "ffn_w_down": nrm(ks[4], (DEPTH, 2, D_FF, D_MODEL), D_FF ** -0.5),
        "w_in": nrm(ks[5], (DEPTH, D_MODEL, d_in), D_MODEL ** -0.5),
        "gla_w_gate2": nrm(ks[6], (DEPTH, GLA_GATE_RANK, GLA_HEADS * GLA_DK), GLA_GATE_RANK ** -0.5),
        "gla_b_gate": nrm(ks[7], (DEPTH, GLA_HEADS * GLA_DK), 0.1),
        "gla_norm_g": 1.0 + nrm(ks[8], (DEPTH, GLA_HEADS, GLA_DV), 0.05),
        "conv_w": nrm(ks[9], (DEPTH, CONV_WIDTH, CONV_CH), CONV_WIDTH ** -0.5),
        "conv_b": nrm(ks[10], (DEPTH, CONV_CH), 0.02),
        "conv_ln_g": 1.0 + nrm(ks[11], (DEPTH, CONV_CH), 0.05),
        "conv_ln_b": nrm(ks[12], (DEPTH, CONV_CH), 0.02),
        "swa_sinks": nrm(ks[13], (DEPTH, SWA_Q_HEADS), 0.5),
        "w_out": nrm(ks[14], (DEPTH, D_MIX, D_MODEL), D_MIX ** -0.5),
    }


def reference(x, norm_g, ffn_w_gate, ffn_w_up, ffn_w_down, w_in, gla_w_gate2, gla_b_gate,
              gla_norm_g, conv_w, conv_b, conv_ln_g, conv_ln_b, swa_sinks, w_out):
    for l in range(DEPTH):
        g = norm_g[l]
        h = swiglu_ffn(rmsnorm(x, g[0]), ffn_w_gate[l, 0], ffn_w_up[l, 0], ffn_w_down[l, 0])
        x = x + 0.5 * rmsnorm(h, g[1])
        h = hybrid_mixer(rmsnorm(x, g[2]), w_in[l], gla_w_gate2[l], gla_b_gate[l], gla_norm_g[l],
                         conv_w[l], conv_b[l], conv_ln_g[l], conv_ln_b[l], swa_sinks[l], w_out[l])
        x = x + rmsnorm(h, g[3])
        h = swiglu_ffn(rmsnorm(x, g[4]), ffn_w_gate[l, 1], ffn_w_up[l, 1], ffn_w_down[l, 1])
        x = x + 0.5 * rmsnorm(h, g[5])
    return x
```

```python
import functools

import jax
import jax.numpy as jnp
from jax import lax
from jax.experimental import pallas as pl
from jax.experimental.pallas import tpu as pltpu

F32 = jnp.float32
BF16 = jnp.bfloat16

D_MODEL = 1024
D_FF = 2816
RMS_EPS = 1e-6
LN_EPS = 1e-5

GLA_HEADS = 4
GLA_DK = 32
GLA_DV = 64
GLA_RANK = 16
GLA_TAU = 16.0
GLA_CHUNK = 64
GLA_SUB = 16
GLA_QK = GLA_HEADS * GLA_DK
GLA_V = GLA_HEADS * GLA_DV

CONV_CH = 256
CONV_WIDTH = 31
CONV_PAD = 32

SWA_Q_HEADS = 8
SWA_KV_HEADS = 2
SWA_GROUP = SWA_Q_HEADS // SWA_KV_HEADS
SWA_DH = 64
SWA_BLOCK = 128
SWA_Q = SWA_Q_HEADS * SWA_DH
SWA_KV2 = SWA_KV_HEADS * 2 * SWA_DH

LANES = 128
FFN_ROWS = 512
FFN_CHUNK = 256
MIX_ROWS = 512
VMEM_LIMIT_BYTES = 56 * 1024 * 1024

W_GLA_COLS = 2 * GLA_QK + 2 * GLA_V + LANES
W_CONV_COLS = 2 * CONV_CH
W_SWA_COLS = SWA_Q + 2 * SWA_KV2


def _rmsnorm(x, g):
    return x * lax.rsqrt(jnp.mean(x * x, axis=-1, keepdims=True) + RMS_EPS) * g


def _dot(a, b):
    return jnp.dot(a, b, preferred_element_type=F32)


def _dot_nt(a, b):
    return lax.dot_general(a, b, (((1,), (1,)), ((), ())), preferred_element_type=F32)


def _dot_tn(a, b):
    return lax.dot_general(a, b, (((0,), (0,)), ((), ())), preferred_element_type=F32)


def _ffn_body(x_ref, gpre_ref, gpost_ref, wg_ref, wu_ref, wd_ref, o_ref, a_ref):
    x = x_ref[...]
    h = _rmsnorm(x, gpre_ref[...]).astype(BF16)
    for c in range(D_FF // FFN_CHUNK):
        sl = slice(c * FFN_CHUNK, (c + 1) * FFN_CHUNK)
        g = _dot(h, wg_ref[:, sl])
        u = _dot(h, wu_ref[:, sl])
        a_ref[:, sl] = (g * jax.nn.sigmoid(g) * u).astype(BF16)
    y = _dot(a_ref[...], wd_ref[...])
    o_ref[...] = x + 0.5 * _rmsnorm(y, gpost_ref[...])


def _resident(shape):
    return pl.BlockSpec(shape, lambda *_: (0,) * len(shape), pipeline_mode=pl.Buffered(1))


def _ffn(x, g_pre, g_post, w_gate, w_up, w_down):
    rows = x.shape[0]
    tm = min(FFN_ROWS, rows)
    return pl.pallas_call(
        _ffn_body,
        grid=(rows // tm,),
        in_specs=[
            pl.BlockSpec((tm, D_MODEL), lambda i: (i, 0)),
            _resident((1, D_MODEL)),
            _resident((1, D_MODEL)),
            _resident((D_MODEL, D_FF)),
            _resident((D_MODEL, D_FF)),
            _resident((D_FF, D_MODEL)),
        ],
        out_specs=pl.BlockSpec((tm, D_MODEL), lambda i: (i, 0)),
        out_shape=jax.ShapeDtypeStruct((rows, D_MODEL), F32),
        scratch_shapes=[pltpu.VMEM((tm, D_FF), BF16)],
        compiler_params=pltpu.CompilerParams(
            dimension_semantics=("arbitrary",), vmem_limit_bytes=VMEM_LIMIT_BYTES),
        name="ffn",
    )(x, g_pre, g_post, w_gate, w_up, w_down)


def _gla_chunk(qc, kc, vc, bc, s_ref):
    c = GLA_CHUNK
    lane_h = lax.broadcasted_iota(jnp.int32, (1, GLA_QK), 1) // GLA_DK
    vlane_h = lax.broadcasted_iota(jnp.int32, (1, GLA_V), 1) // GLA_DV
    rowi = lax.broadcasted_iota(jnp.int32, (c, 1), 0)
    colj = lax.broadcasted_iota(jnp.int32, (c, GLA_HEADS * GLA_SUB), 1) % GLA_SUB
    b_last = bc[c - 1:c]

    o = _dot((qc * jnp.exp(bc)).astype(BF16), s_ref[...].astype(BF16))

    a_parts, v_parts = [], []
    for j0 in range(0, c, GLA_SUB):
        e_j = bc[j0 + GLA_SUB - 1:j0 + GLA_SUB]
        k_j = kc[j0:j0 + GLA_SUB] * jnp.exp(e_j - bc[j0:j0 + GLA_SUB])
        k_stack = jnp.concatenate(
            [jnp.where(lane_h == h, k_j, 0.0) for h in range(GLA_HEADS)], axis=0).astype(BF16)
        q_j = (qc * jnp.exp(jnp.where(rowi >= j0, bc - e_j, 0.0))).astype(BF16)
        a_j = _dot_nt(q_j, k_stack)
        a_parts.append(jnp.where(rowi >= j0 + colj, a_j, 0.0))
        v_j = vc[j0:j0 + GLA_SUB]
        v_parts += [jnp.where(vlane_h == h, v_j, 0.0) for h in range(GLA_HEADS)]
    a_all = jnp.concatenate(a_parts, axis=1).astype(BF16)
    v_bd = jnp.concatenate(v_parts, axis=0).astype(BF16)
    o = o + _dot(a_all, v_bd)

    k_l = (kc * jnp.exp(b_last - bc)).astype(BF16)
    upd = _dot_tn(k_l, vc.astype(BF16))
    row_h = lax.broadcasted_iota(jnp.int32, (GLA_QK, GLA_V), 0) // GLA_DK
    col_h = lax.broadcasted_iota(jnp.int32, (GLA_QK, GLA_V), 1) // GLA_DV
    decay_col = jnp.transpose(jnp.exp(bc[c - 8:c]))[:, 7:8]
    s_ref[...] = decay_col * s_ref[...] + jnp.where(row_h == col_h, upd, 0.0)
    return o


def _mixer_body(x_ref, gpre_ref, gpost_ref, wgla_ref, wgate2_ref, bgate_ref, gnorm_ref,
                wconv_ref, convw_ref, convp_ref, wswa_ref, sinks_ref, wout_ref,
                o_ref,
                s_ref, cext_ref, kprev_ref, vprev_ref, cat_ref):
    ts = x_ref.shape[0]
    t = pl.program_id(1)

    @pl.when(t == 0)
    def _():
        s_ref[...] = jnp.zeros_like(s_ref)
        cext_ref[0:CONV_PAD, :] = jnp.zeros((CONV_PAD, CONV_CH), F32)
        kprev_ref[...] = jnp.zeros_like(kprev_ref)
        vprev_ref[...] = jnp.zeros_like(vprev_ref)

    x = x_ref[...]
    h = _rmsnorm(x, gpre_ref[...]).astype(BF16)

    pg = _dot(h, wgla_ref[...])
    q = pg[:, 0:GLA_QK] * (GLA_DK ** -0.5)
    k = pg[:, GLA_QK:2 * GLA_QK]
    v = pg[:, 2 * GLA_QK:2 * GLA_QK + GLA_V]
    r = pg[:, 2 * GLA_QK + GLA_V:2 * GLA_QK + 2 * GLA_V]
    glr = pg[:, 2 * GLA_QK + 2 * GLA_V:]
    z = _dot(glr.astype(BF16), wgate2_ref[...]) + bgate_ref[...]
    glog = jax.nn.log_sigmoid(z) / GLA_TAU
    row_in_chunk = lax.broadcasted_iota(jnp.int32, (ts, GLA_QK), 0) % GLA_CHUNK
    b = glog
    shift = 1
    while shift < GLA_CHUNK:
        b = b + jnp.where(row_in_chunk >= shift, pltpu.roll(b, shift, axis=0), 0.0)
        shift *= 2
    o_parts = []
    for c0 in range(0, ts, GLA_CHUNK):
        sl = slice(c0, c0 + GLA_CHUNK)
        o_parts.append(_gla_chunk(q[sl], k[sl], v[sl], b[sl], s_ref))
    o = jnp.concatenate(o_parts, axis=0)
    vlane_h = lax.broadcasted_iota(jnp.int32, (1, GLA_V), 1) // GLA_DV
    oo = o * o
    ms = jnp.zeros_like(o)
    for hd in range(GLA_HEADS):
        sel = vlane_h == hd
        ms_h = jnp.sum(jnp.where(sel, oo, 0.0), axis=-1, keepdims=True) * (1.0 / GLA_DV)
        ms = jnp.where(sel, ms_h, ms)
    o = o * lax.rsqrt(ms + RMS_EPS) * gnorm_ref[...]
    cat_ref[:, 0:GLA_V] = (o * (r * jax.nn.sigmoid(r))).astype(BF16)

    pc = _dot(h, wconv_ref[...])
    u = pc[:, 0:CONV_CH] * jax.nn.sigmoid(pc[:, CONV_CH:])
    cext_ref[CONV_PAD:CONV_PAD + ts, :] = u
    y = jnp.zeros((ts, CONV_CH), F32) + convp_ref[0:1, :]
    for w in range(CONV_WIDTH):
        lo = CONV_PAD - (CONV_WIDTH - 1 - w)
        y = y + convw_ref[w:w + 1, :] * cext_ref[lo:lo + ts, :]
    cext_ref[0:CONV_PAD, :] = cext_ref[ts:ts + CONV_PAD, :]
    mu = jnp.mean(y, axis=-1, keepdims=True)
    yc = y - mu
    var = jnp.mean(yc * yc, axis=-1, keepdims=True)
    yn = yc * lax.rsqrt(var + LN_EPS) * convp_ref[1:2, :] + convp_ref[2:3, :]
    cat_ref[:, GLA_V:GLA_V + CONV_CH] = (yn * jax.nn.sigmoid(yn)).astype(BF16)

    ps = _dot(h, wswa_ref[...])
    qs = (ps[:, 0:SWA_Q] * (SWA_DH ** -0.5)).astype(BF16)
    k_cur = ps[:, SWA_Q:SWA_Q + SWA_KV2].astype(BF16)
    v_cur = ps[:, SWA_Q + SWA_KV2:].astype(BF16)
    k_all = jnp.concatenate([kprev_ref[...].astype(BF16), k_cur], axis=0)
    v_all = jnp.concatenate([vprev_ref[...].astype(BF16), v_cur], axis=0)
    kprev_ref[...] = ps[ts - SWA_BLOCK:, SWA_Q:SWA_Q + SWA_KV2]
    vprev_ref[...] = ps[ts - SWA_BLOCK:, SWA_Q + SWA_KV2:]
    qi = lax.broadcasted_iota(jnp.int32, (SWA_BLOCK, 2 * SWA_BLOCK), 0)
    kj = lax.broadcasted_iota(jnp.int32, (SWA_BLOCK, 2 * SWA_BLOCK), 1)
    dist = qi - kj + SWA_BLOCK
    in_window = (dist >= 0) & (dist < SWA_BLOCK)
    dist_f = dist.astype(F32)
    low_half = lax.broadcasted_iota(jnp.int32, (1, LANES), 1) < SWA_DH
    for n in range(ts // SWA_BLOCK):
        rows = slice(n * SWA_BLOCK, (n + 1) * SWA_BLOCK)
        valid = in_window
        if n == 0:
            valid = valid & (kj >= jnp.where(t > 0, 0, SWA_BLOCK))
        for hk in range(SWA_KV_HEADS):
            kk = k_all[n * SWA_BLOCK:(n + 2) * SWA_BLOCK, hk * LANES:(hk + 1) * LANES]
            vv = v_all[n * SWA_BLOCK:(n + 2) * SWA_BLOCK, hk * LANES:(hk + 1) * LANES]
            q_stack = []
            for g in range(SWA_GROUP):
                col = (hk * SWA_GROUP + g) // 2 * LANES
                q_pair = qs[rows, col:col + LANES]
                keep = low_half if g % 2 == 0 else ~low_half
                q_stack.append(jnp.where(keep, q_pair, jnp.zeros_like(q_pair)))
            s_all = _dot_nt(jnp.concatenate(q_stack, axis=0), kk)
            p_stack, inv_l = [], []
            for g in range(SWA_GROUP):
                hq = hk * SWA_GROUP + g
                slope = 2.0 ** (-8.0 * (hq + 1) / SWA_Q_HEADS)
                s_g = s_all[g * SWA_BLOCK:(g + 1) * SWA_BLOCK] - slope * dist_f
                s_g = jnp.where(valid, s_g, -jnp.inf)
                sink = sinks_ref[hq:hq + 1, 0:1]
                m = jnp.maximum(jnp.max(s_g, axis=-1, keepdims=True), sink)
                p = jnp.exp(s_g - m)
                inv_l.append(1.0 / (jnp.sum(p, axis=-1, keepdims=True) + jnp.exp(sink - m)))
                p_stack.append(p.astype(BF16))
            o_all = _dot(jnp.concatenate(p_stack, axis=0), vv)
            for pair in range(SWA_GROUP // 2):
                g0, g1 = 2 * pair, 2 * pair + 1
                o0 = o_all[g0 * SWA_BLOCK:(g0 + 1) * SWA_BLOCK] * inv_l[g0]
                o1 = o_all[g1 * SWA_BLOCK:(g1 + 1) * SWA_BLOCK] * inv_l[g1]
                col = GLA_V + CONV_CH + (hk * SWA_GROUP + g0) // 2 * LANES
                cat_ref[rows, col:col + LANES] = jnp.where(low_half, o0, o1).astype(BF16)

    y_out = _dot(cat_ref[...], wout_ref[...])
    o_ref[...] = x + _rmsnorm(y_out, gpost_ref[...])


def _prep_mixer_weights(w_in, gla_w_gate2, gla_b_gate, gla_norm_g, conv_w, conv_b, conv_ln_g,
                        conv_ln_b, swa_sinks, w_out):
    widths = [GLA_QK, GLA_QK, GLA_V, GLA_V, GLA_RANK, CONV_CH, CONV_CH, SWA_Q,
              SWA_KV_HEADS * SWA_DH, SWA_KV_HEADS * SWA_DH]
    offs = [0]
    for w in widths:
        offs.append(offs[-1] + w)
    gq, gk, gv, gr, glr, ca, cg, sq, sk, sv = [w_in[:, offs[i]:offs[i + 1]] for i in range(10)]
    d = w_in.shape[0]
    w_gla = jnp.concatenate([gq, gk, gv, gr, glr, jnp.zeros((d, LANES - GLA_RANK), F32)], axis=1)
    w_conv = jnp.concatenate([ca, cg], axis=1)
    dup = lambda m: jnp.concatenate(
        [m[:, hh * SWA_DH:(hh + 1) * SWA_DH] for hh in range(SWA_KV_HEADS) for _ in range(2)], axis=1)
    w_swa = jnp.concatenate([sq, dup(sk), dup(sv)], axis=1)
    w_gate2 = jnp.concatenate([gla_w_gate2, jnp.zeros((LANES - GLA_RANK, GLA_QK), F32)], axis=0)
    conv_p = jnp.concatenate([conv_b[None], conv_ln_g[None], conv_ln_b[None],
                              jnp.zeros((5, CONV_CH), F32)], axis=0)
    conv_wp = jnp.concatenate([conv_w, jnp.zeros((1, CONV_CH), F32)], axis=0)
    return dict(
        w_gla=w_gla.astype(BF16), w_gate2=w_gate2.astype(BF16), b_gate=gla_b_gate[None],
        g_norm=gla_norm_g.reshape(1, GLA_V), w_conv=w_conv.astype(BF16), conv_w=conv_wp,
        conv_p=conv_p, w_swa=w_swa.astype(BF16),
        sinks=jnp.broadcast_to(swa_sinks[:, None], (SWA_Q_HEADS, LANES)),
        w_out=w_out.astype(BF16))


def _mixer(x, batch, g_pre, g_post, p):
    rows = x.shape[0]
    seq = rows // batch
    ts = min(MIX_ROWS, seq)
    nt = seq // ts
    assert seq % ts == 0 and ts % SWA_BLOCK == 0 and ts % GLA_CHUNK == 0
    row_block = pl.BlockSpec((ts, D_MODEL), lambda b, t: (b * nt + t, 0))
    return pl.pallas_call(
        _mixer_body,
        grid=(batch, nt),
        in_specs=[
            row_block,
            _resident((1, D_MODEL)),
            _resident((1, D_MODEL)),
            _resident((D_MODEL, W_GLA_COLS)),
            _resident((LANES, GLA_QK)),
            _resident((1, GLA_QK)),
            _resident((1, GLA_V)),
            _resident((D_MODEL, W_CONV_COLS)),
            _resident((CONV_WIDTH + 1, CONV_CH)),
            _resident((8, CONV_CH)),
            _resident((D_MODEL, W_SWA_COLS)),
            _resident((SWA_Q_HEADS, LANES)),
            _resident((D_MODEL, D_MODEL)),
        ],
        out_specs=row_block,
        out_shape=jax.ShapeDtypeStruct((rows, D_MODEL), F32),
        scratch_shapes=[
            pltpu.VMEM((GLA_QK, GLA_V), F32),
            pltpu.VMEM((CONV_PAD + ts, CONV_CH), F32),
            pltpu.VMEM((SWA_BLOCK, SWA_KV2), F32),
            pltpu.VMEM((SWA_BLOCK, SWA_KV2), F32),
            pltpu.VMEM((ts, D_MODEL), BF16),
        ],
        compiler_params=pltpu.CompilerParams(
            dimension_semantics=("arbitrary", "arbitrary"), vmem_limit_bytes=VMEM_LIMIT_BYTES),
        name="mixer",
    )(x, g_pre, g_post, p["w_gla"], p["w_gate2"], p["b_gate"], p["g_norm"], p["w_conv"],
      p["conv_w"], p["conv_p"], p["w_swa"], p["sinks"], p["w_out"])


def kernel(x, norm_g, ffn_w_gate, ffn_w_up, ffn_w_down, w_in, gla_w_gate2, gla_b_gate, gla_norm_g, conv_w, conv_b, conv_ln_g, conv_ln_b, swa_sinks, w_out):
    batch, seq, d = x.shape
    xf = x.reshape(batch * seq, d)
    for l in range(norm_g.shape[0]):
        g = norm_g[l]
        xf = _ffn(xf, g[0:1], g[1:2], ffn_w_gate[l, 0].astype(BF16),
                  ffn_w_up[l, 0].astype(BF16), ffn_w_down[l, 0].astype(BF16))
        p = _prep_mixer_weights(w_in[l], gla_w_gate2[l], gla_b_gate[l], gla_norm_g[l], conv_w[l],
                                conv_b[l], conv_ln_g[l], conv_ln_b[l], swa_sinks[l], w_out[l])
        xf = _mixer(xf, batch, g[2:3], g[3:4], p)
        xf = _ffn(xf, g[4:5], g[5:6], ffn_w_gate[l, 1].astype(BF16),
                  ffn_w_up[l, 1].astype(BF16), ffn_w_down[l, 1].astype(BF16))
    return xf.reshape(batch, seq, d)
```

```python
import math

import jax
import jax.numpy as jnp
import numpy as np
from jax import lax
from jax.experimental import pallas as pl
from jax.experimental.pallas import tpu as pltpu

F32 = jnp.float32
BF16 = jnp.bfloat16

D_MODEL = 1024
D_FF = 2816
RMS_EPS = 1e-6
LN_EPS = 1e-5

GLA_HEADS = 4
GLA_DK = 32
GLA_DV = 64
GLA_RANK = 16
GLA_TAU = 16.0
GLA_CHUNK = 64
GLA_SUB = 16
GLA_QK = GLA_HEADS * GLA_DK
GLA_V = GLA_HEADS * GLA_DV

CONV_CH = 256
CONV_WIDTH = 31
CONV_PAD = 32
CONV_ROWS = 128

SWA_Q_HEADS = 8
SWA_KV_HEADS = 2
SWA_GROUP = SWA_Q_HEADS // SWA_KV_HEADS
SWA_DH = 64
SWA_BLOCK = 128
SWA_Q = SWA_Q_HEADS * SWA_DH
SWA_KV2 = SWA_KV_HEADS * 2 * SWA_DH
LOG2E = math.log2(math.e)

SUBLANES = 8
LANES = 128
FFN_ROWS = 512
FFN_CHUNK = 256
MIX_ROWS = 512
VMEM_LIMIT_BYTES = 56 * 1024 * 1024

W_GLA_COLS = 2 * GLA_QK + 2 * GLA_V + LANES
W_CONV_COLS = 2 * CONV_CH
W_SWA_KV_COLS = 2 * SWA_KV2


def _rmsnorm(x, g):
    return x * lax.rsqrt(jnp.mean(x * x, axis=-1, keepdims=True) + RMS_EPS) * g


def _dot(a, b):
    return jnp.dot(a, b, preferred_element_type=F32)


def _dot_nt(a, b):
    return lax.dot_general(a, b, (((1,), (1,)), ((), ())), preferred_element_type=F32)


def _dot_tn(a, b):
    return lax.dot_general(a, b, (((0,), (0,)), ((), ())), preferred_element_type=F32)


def _ffn_body(x_ref, gpre_ref, gpost_ref, wg_ref, wu_ref, wd_ref, o_ref, a_ref):
    x = x_ref[...]
    h = _rmsnorm(x, gpre_ref[...]).astype(BF16)
    for c in range(D_FF // FFN_CHUNK):
        sl = slice(c * FFN_CHUNK, (c + 1) * FFN_CHUNK)
        g = _dot(h, wg_ref[:, sl])
        u = _dot(h, wu_ref[:, sl])
        a_ref[:, sl] = (g * jax.nn.sigmoid(g) * u).astype(BF16)
    y = _dot(a_ref[...], wd_ref[...])
    o_ref[...] = x + 0.5 * _rmsnorm(y, gpost_ref[...])


def _resident(shape):
    return pl.BlockSpec(shape, lambda *_: (0,) * len(shape), pipeline_mode=pl.Buffered(1))


def _ffn(x, g_pre, g_post, w_gate, w_up, w_down):
    rows = x.shape[0]
    tm = min(FFN_ROWS, rows)
    return pl.pallas_call(
        _ffn_body,
        grid=(rows // tm,),
        in_specs=[
            pl.BlockSpec((tm, D_MODEL), lambda i: (i, 0)),
            _resident((1, D_MODEL)),
            _resident((1, D_MODEL)),
            _resident((D_MODEL, D_FF)),
            _resident((D_MODEL, D_FF)),
            _resident((D_FF, D_MODEL)),
        ],
        out_specs=pl.BlockSpec((tm, D_MODEL), lambda i: (i, 0)),
        out_shape=jax.ShapeDtypeStruct((rows, D_MODEL), F32),
        scratch_shapes=[pltpu.VMEM((tm, D_FF), BF16)],
        compiler_params=pltpu.CompilerParams(
            dimension_semantics=("arbitrary",), vmem_limit_bytes=VMEM_LIMIT_BYTES),
        name="ffn",
    )(x, g_pre, g_post, w_gate, w_up, w_down)


def _gla(q, k, v, b, s_ref):
    ts = q.shape[0]
    c, sub = GLA_CHUNK, GLA_SUB
    chunks = range(0, ts, c)
    lane_h = lax.broadcasted_iota(jnp.int32, (1, GLA_QK), 1) // GLA_DK
    vlane_h = lax.broadcasted_iota(jnp.int32, (1, GLA_V), 1) // GLA_DV
    colj = lax.broadcasted_iota(jnp.int32, (c, GLA_HEADS * sub), 1) % sub
    rowi = lax.broadcasted_iota(jnp.int32, (c, 1), 0)
    row_h =lax.broadcasted_iota(jnp.int32, (GLA_QK, GLA_V), 0) // GLA_DK
    col_h = lax.broadcasted_iota(jnp.int32, (GLA_QK, GLA_V), 1) // GLA_DV
    on_diag = row_h == col_h

    upd, decay_col, q_in, a_all, v_bd = [], [], [], [], []
    for c0 in chunks:
        qc, kc, vc, bc = q[c0:c0 + c], k[c0:c0 + c], v[c0:c0 + c], b[c0:c0 + c]
        b_last = bc[c - 1:c]
        q_in.append((qc * jnp.exp(bc)).astype(BF16))
        k_l = (kc * jnp.exp(b_last - bc)).astype(BF16)
        upd.append(_dot_tn(k_l, vc.astype(BF16)))
        decay_col.append(jnp.transpose(jnp.exp(bc[c - SUBLANES:c]))[:, SUBLANES - 1:SUBLANES])
        a_parts, v_parts = [], []
        for j0 in range(0, c, sub):
            e_j = bc[j0 + sub - 1:j0 + sub]
            k_j = kc[j0:j0 + sub] * jnp.exp(e_j - bc[j0:j0 + sub])
            k_stack = jnp.concatenate(
                [jnp.where(lane_h == h, k_j, 0.0) for h in range(GLA_HEADS)], axis=0).astype(BF16)
            q_j = (qc * jnp.exp(jnp.where(rowi >= j0, bc - e_j, 0.0))).astype(BF16)
            a_j = _dot_nt(q_j, k_stack)
            a_parts.append(jnp.where(rowi >= j0 + colj, a_j, 0.0))
            v_j = vc[j0:j0 + sub]
            v_parts += [jnp.where(vlane_h == h, v_j, 0.0) for h in range(GLA_HEADS)]
        a_all.append(jnp.concatenate(a_parts, axis=1).astype(BF16))
        v_bd.append(jnp.concatenate(v_parts, axis=0).astype(BF16))

    states = []
    s = s_ref[...]
    for i in range(len(upd)):
        states.append(s.astype(BF16))
        s = decay_col[i] * s + jnp.where(on_diag, upd[i], 0.0)
    s_ref[...] = s

    o = [_dot(q_in[i], states[i]) + _dot(a_all[i], v_bd[i]) for i in range(len(upd))]
    return jnp.concatenate(o, axis=0)


def _mixer_body(x_ref, gpre_ref, gpost_ref, wconv_ref, wkv_ref, wq_ref, wgla_ref, wgate2_ref,
                bgate_ref, gnorm_ref, convw_ref, convp_ref, bias_ref, sinks_ref, wout_ref,
                o_ref,
                s_ref, csh_ref, kprev_ref, vprev_ref, cat_ref):
    ts = x_ref.shape[0]
    t = pl.program_id(1)

    @pl.when(t == 0)
    def _():
        s_ref[...] = jnp.zeros_like(s_ref)
        csh_ref[0, 0:CONV_PAD, :] = jnp.zeros((CONV_PAD, CONV_CH), F32)
        kprev_ref[...] = jnp.zeros_like(kprev_ref)
        vprev_ref[...] = jnp.zeros_like(vprev_ref)

    x = x_ref[...]
    h = _rmsnorm(x, gpre_ref[...]).astype(BF16)

    pc = _dot(h, wconv_ref[...])
    csh_ref[0, CONV_PAD:CONV_PAD + ts, :] = pc[:, 0:CONV_CH] * jax.nn.sigmoid(pc[:, CONV_CH:])
    ext = csh_ref[0]
    for r in range(1, SUBLANES):
        csh_ref[r] = pltpu.roll(ext, r, axis=0)

    pkv = _dot(h, wkv_ref[...])
    qs = (_dot(h, wq_ref[...]) * (SWA_DH ** -0.5 * LOG2E)).astype(BF16)
    k_cur = pkv[:, 0:SWA_KV2].astype(BF16)
    v_cur = pkv[:, SWA_KV2:].astype(BF16)
    k_all = jnp.concatenate([kprev_ref[...].astype(BF16), k_cur], axis=0)
    v_all = jnp.concatenate([vprev_ref[...].astype(BF16), v_cur], axis=0)
    kprev_ref[...] = pkv[ts - SWA_BLOCK:, 0:SWA_KV2]
    vprev_ref[...] = pkv[ts - SWA_BLOCK:, SWA_KV2:]

    for i0 in range(0, ts, CONV_ROWS):
        y = jnp.zeros((CONV_ROWS, CONV_CH), F32) + convp_ref[0:1, :]
        for back in range(CONV_WIDTH):
            a, r = divmod(back, SUBLANES)
            lo = CONV_PAD - SUBLANES * a + i0
            w = CONV_WIDTH - 1 - back
            y = y + convw_ref[w:w + 1, :] * csh_ref[r, lo:lo + CONV_ROWS, :]
        mu = jnp.mean(y, axis=-1, keepdims=True)
        yc = y - mu
        var = jnp.mean(yc * yc, axis=-1, keepdims=True)
        yn = yc * lax.rsqrt(var + LN_EPS) * convp_ref[1:2, :] + convp_ref[2:3, :]
        cat_ref[i0:i0 + CONV_ROWS, GLA_V:GLA_V + CONV_CH] = (yn * jax.nn.sigmoid(yn)).astype(BF16)
    csh_ref[0, 0:CONV_PAD, :] = csh_ref[0, ts:ts + CONV_PAD, :]

    low_half = lax.broadcasted_iota(jnp.int32, (1, LANES), 1) < SWA_DH
    ones = jnp.ones((2 * SWA_BLOCK, LANES), BF16)
    items = [(n, hk) for n in range(ts // SWA_BLOCK) for hk in range(SWA_KV_HEADS)]
    scores = []
    for n, hk in items:
        rows = slice(n * SWA_BLOCK, (n + 1) * SWA_BLOCK)
        kk = k_all[n * SWA_BLOCK:(n + 2) * SWA_BLOCK, hk * LANES:(hk + 1) * LANES]
        q_stack = []
        for g in range(SWA_GROUP):
            col = (hk * SWA_GROUP + g) // 2 * LANES
            q_pair = qs[rows, col:col + LANES]
            keep = low_half if g % 2 == 0 else ~low_half
            q_stack.append(jnp.where(keep, q_pair, jnp.zeros_like(q_pair)))
        scores.append(_dot_nt(jnp.concatenate(q_stack, axis=0), kk))

    pg = _dot(h, wgla_ref[...])
    r_gate = pg[:, 2 * GLA_QK + GLA_V:2 * GLA_QK + 2 * GLA_V]
    z = _dot(pg[:, 2 * GLA_QK + 2 * GLA_V:].astype(BF16), wgate2_ref[...]) + bgate_ref[...]
    glog = jax.nn.log_sigmoid(z) / GLA_TAU
    row_in_chunk = lax.broadcasted_iota(jnp.int32, (ts, GLA_QK), 0) % GLA_CHUNK
    b = glog
    shift = 1
    while shift < GLA_CHUNK:
        b = b + jnp.where(row_in_chunk >= shift, pltpu.roll(b, shift, axis=0), 0.0)
        shift *= 2
    o_gla = _gla(pg[:, 0:GLA_QK] * (GLA_DK ** -0.5), pg[:, GLA_QK:2 * GLA_QK],
                 pg[:, 2 * GLA_QK:2 * GLA_QK + GLA_V], b, s_ref)

    outs = []
    row_group = lax.broadcasted_iota(jnp.int32, (SWA_GROUP * SWA_BLOCK, 1), 0) // SWA_BLOCK
    for (n, hk), s_all in zip(items, scores):
        first = jnp.where(t == 0, 1, 0) if n == 0 else 0
        s_all = s_all - bias_ref[first * SWA_KV_HEADS + hk]
        sink = sinks_ref[hk * SWA_GROUP:hk * SWA_GROUP + 1, 0:1]
        for g in range(1, SWA_GROUP):
            sink = jnp.where(row_group >= g, sinks_ref[hk * SWA_GROUP + g:hk * SWA_GROUP + g + 1, 0:1], sink)
        m = jnp.maximum(jnp.max(s_all, axis=-1, keepdims=True), sink)
        p = jnp.exp2(s_all - m).astype(BF16)
        vv = v_all[n * SWA_BLOCK:(n + 2) * SWA_BLOCK, hk * LANES:(hk + 1) * LANES]
        o_ext = _dot(p, jnp.concatenate([vv, ones], axis=1))
        outs.append((o_ext, jnp.exp2(sink - m)))

    vlane_h = lax.broadcasted_iota(jnp.int32, (1, GLA_V), 1) // GLA_DV
    oo = o_gla * o_gla
    ms = jnp.zeros_like(o_gla)
    for hd in range(GLA_HEADS):
        sel = vlane_h == hd
        ms_h = jnp.sum(jnp.where(sel, oo, 0.0), axis=-1, keepdims=True) * (1.0 / GLA_DV)
        ms = jnp.where(sel, ms_h, ms)
    o_gla = o_gla * lax.rsqrt(ms + RMS_EPS) * gnorm_ref[...]
    cat_ref[:, 0:GLA_V] = (o_gla * (r_gate * jax.nn.sigmoid(r_gate))).astype(BF16)

    for (n, hk), (o_ext, e_sink) in zip(items, outs):
        rows = slice(n * SWA_BLOCK, (n + 1) * SWA_BLOCK)
        o_n = o_ext[:, 0:LANES] / (o_ext[:, LANES:] + e_sink)
        for pair in range(SWA_GROUP // 2):
            g0, g1 = 2 * pair, 2 * pair + 1
            col = GLA_V + CONV_CH + (hk * SWA_GROUP + g0) // 2 * LANES
            cat_ref[rows, col:col + LANES] = jnp.where(
                low_half, o_n[g0 * SWA_BLOCK:(g0 + 1) * SWA_BLOCK],
                o_n[g1 * SWA_BLOCK:(g1 + 1) * SWA_BLOCK]).astype(BF16)

    y_out = _dot(cat_ref[...], wout_ref[...])
    o_ref[...] = x + _rmsnorm(y_out, gpost_ref[...])


def _swa_bias_table():
    qi = np.arange(SWA_BLOCK)[:, None]
    kj = np.arange(2 * SWA_BLOCK)[None, :]
    dist = qi - kj + SWA_BLOCK
    valid = (dist >= 0) & (dist < SWA_BLOCK)
    tables = []
    for first in (False, True):
        ok = valid & (kj >= SWA_BLOCK) if first else valid
        for hk in range(SWA_KV_HEADS):
            blocks = []
            for g in range(SWA_GROUP):
                slope = 2.0 ** (-8.0 * (hk * SWA_GROUP + g + 1) / SWA_Q_HEADS)
                blocks.append(np.where(ok, slope * LOG2E * dist, np.inf))
            tables.append(np.concatenate(blocks, axis=0))
    return jnp.asarray(np.stack(tables), F32)


def _prep_mixer_weights(w_in, gla_w_gate2, gla_b_gate, gla_norm_g, conv_w, conv_b, conv_ln_g,
                        conv_ln_b, swa_sinks, w_out):
    widths = [GLA_QK, GLA_QK, GLA_V, GLA_V, GLA_RANK, CONV_CH, CONV_CH, SWA_Q,
              SWA_KV_HEADS * SWA_DH, SWA_KV_HEADS * SWA_DH]
    offs = [0]
    for w in widths:
        offs.append(offs[-1] + w)
    gq, gk, gv, gr, glr, ca, cg, sq, sk, sv = [w_in[:, offs[i]:offs[i + 1]] for i in range(10)]
    d = w_in.shape[0]
    w_gla = jnp.concatenate([gq, gk, gv, gr, glr, jnp.zeros((d, LANES - GLA_RANK), F32)], axis=1)
    w_conv = jnp.concatenate([ca, cg], axis=1)
    dup = lambda m: jnp.concatenate(
        [m[:, hh * SWA_DH:(hh + 1) * SWA_DH] for hh in range(SWA_KV_HEADS) for _ in range(2)], axis=1)
    w_kv = jnp.concatenate([dup(sk), dup(sv)], axis=1)
    w_gate2 = jnp.concatenate([gla_w_gate2, jnp.zeros((LANES - GLA_RANK, GLA_QK), F32)], axis=0)
    conv_p = jnp.concatenate([conv_b[None], conv_ln_g[None], conv_ln_b[None],
                              jnp.zeros((5, CONV_CH), F32)], axis=0)
    conv_wp = jnp.concatenate([conv_w, jnp.zeros((1, CONV_CH), F32)], axis=0)
    return dict(
        w_conv=w_conv.astype(BF16), w_kv=w_kv.astype(BF16), w_q=sq.astype(BF16),
        w_gla=w_gla.astype(BF16), w_gate2=w_gate2.astype(BF16), b_gate=gla_b_gate[None],
        g_norm=gla_norm_g.reshape(1, GLA_V), conv_w=conv_wp, conv_p=conv_p,
        bias=_swa_bias_table(),
        sinks=jnp.broadcast_to(swa_sinks[:, None] * LOG2E, (SWA_Q_HEADS, LANES)),
        w_out=w_out.astype(BF16))


def _mixer(x, batch, g_pre, g_post, p):
    rows = x.shape[0]
    seq = rows // batch
    ts = min(MIX_ROWS, seq)
    nt = seq // ts
    assert seq % ts == 0 and ts % SWA_BLOCK == 0 and ts % GLA_CHUNK == 0 and ts % CONV_ROWS == 0
    row_block = pl.BlockSpec((ts, D_MODEL), lambda b, t: (b * nt + t, 0))
    return pl.pallas_call(
        _mixer_body,
        grid=(batch, nt),
        in_specs=[
            row_block,
            _resident((1, D_MODEL)),
            _resident((1, D_MODEL)),
            _resident((D_MODEL, W_CONV_COLS)),
            _resident((D_MODEL, W_SWA_KV_COLS)),
            _resident((D_MODEL, SWA_Q)),
            _resident((D_MODEL, W_GLA_COLS)),
            _resident((LANES, GLA_QK)),
            _resident((1, GLA_QK)),
            _resident((1, GLA_V)),
            _resident((CONV_WIDTH + 1, CONV_CH)),
            _resident((SUBLANES, CONV_CH)),
            _resident((2 * SWA_KV_HEADS, SWA_GROUP * SWA_BLOCK, 2 * SWA_BLOCK)),
            _resident((SWA_Q_HEADS, LANES)),
            _resident((D_MODEL, D_MODEL)),
        ],
        out_specs=row_block,
        out_shape=jax.ShapeDtypeStruct((rows, D_MODEL), F32),
        scratch_shapes=[
            pltpu.VMEM((GLA_QK, GLA_V), F32),
            pltpu.VMEM((SUBLANES, CONV_PAD + ts, CONV_CH), F32),
            pltpu.VMEM((SWA_BLOCK, SWA_KV2), F32),
            pltpu.VMEM((SWA_BLOCK, SWA_KV2), F32),
            pltpu.VMEM((ts, D_MODEL), BF16),
        ],
        compiler_params=pltpu.CompilerParams(
            dimension_semantics=("arbitrary", "arbitrary"), vmem_limit_bytes=VMEM_LIMIT_BYTES),
        name="mixer",
    )(x, g_pre, g_post, p["w_conv"], p["w_kv"], p["w_q"], p["w_gla"], p["w_gate2"], p["b_gate"],
      p["g_norm"], p["conv_w"], p["conv_p"], p["bias"], p["sinks"], p["w_out"])


def kernel(x, norm_g, ffn_w_gate, ffn_w_up, ffn_w_down, w_in, gla_w_gate2, gla_b_gate, gla_norm_g, conv_w, conv_b, conv_ln_g, conv_ln_b, swa_sinks, w_out):
    batch, seq, d = x.shape
    xf = x.reshape(batch * seq, d)
    for l in range(norm_g.shape[0]):
        g = norm_g[l]
        xf = _ffn(xf, g[0:1], g[1:2], ffn_w_gate[l, 0].astype(BF16),
                  ffn_w_up[l, 0].astype(BF16), ffn_w_down[l, 0].astype(BF16))
        p = _prep_mixer_weights(w_in[l], gla_w_gate2[l], gla_b_gate[l], gla_norm_g[l], conv_w[l],
                                conv_b[l], conv_ln_g[l], conv_ln_b[l], swa_sinks[l], w_out[l])
        xf = _mixer(xf, batch, g[2:3], g[3:4], p)
        xf = _ffn(xf, g[4:5], g[5:6], ffn_w_gate[l, 1].astype(BF16),
                  ffn_w_up[l, 1].astype(BF16), ffn_w_down[l, 1].astype(BF16))
    return xf.reshape(batch, seq, d)
```
